```python
import math
import jax
import jax.numpy as jnp
from jax import lax

D_MODEL = 1024
BATCH = 2
SEQ = 16384
DEPTH = 4

N_META = 16
CHUNK = 64
NORM_EPS = 1e-6

GLA_HEADS = 4
GLA_DK = D_MODEL // (2 * GLA_HEADS)
GLA_DV = D_MODEL // GLA_HEADS
GLA_GATE_RANK = 16
GLA_GATE_NORMALIZER = 16.0
GLA_QK = GLA_HEADS * GLA_DK
GLA_V = GLA_HEADS * GLA_DV
GLA_IN = 2 * GLA_QK + 2 * GLA_V + GLA_GATE_RANK

GDN_HEADS = 8
GDN_DK = D_MODEL // GDN_HEADS
GDN_DV = 2 * GDN_DK
GDN_QK = GDN_HEADS * GDN_DK
GDN_V = GDN_HEADS * GDN_DV
GDN_IN = 2 * GDN_QK + 2 * GDN_V + 2 * GDN_HEADS
CONV_WIDTH = 4

N_GLA = (DEPTH + 1) // 2
N_GDN = DEPTH // 2

kernel_name = 'hybrid_gla_gdn_meta'


def rms_norm(x, w):
    xf = x.astype(jnp.float32)
    y = xf * lax.rsqrt(jnp.mean(xf * xf, axis=-1, keepdims=True) + NORM_EPS)
    return (y * w.astype(jnp.float32)).astype(x.dtype)


def l2_norm(x):
    return x * lax.rsqrt(jnp.sum(x * x, axis=-1, keepdims=True) + NORM_EPS)


def to_chunks(t, c):
    b, l, h, d = t.shape
    return t.reshape(b, l // c, c, h, d).transpose(0, 3, 1, 2, 4)


def from_chunks(t):
    b, h, n, c, d = t.shape
    return t.transpose(0, 2, 3, 1, 4).reshape(b, n * c, h, d)


def chunk_first(t):
    return jnp.moveaxis(t, 2, 0)


def causal_depthwise_conv(x, w):
    k = w.shape[0]
    return lax.conv_general_dilated(
        x, w[:, None, :].astype(x.dtype), window_strides=(1,), padding=[(k - 1, 0)],
        dimension_numbers=('NWC', 'WIO', 'NWC'), feature_group_count=x.shape[-1])


def gla_chunk_scan(q, k, v, log_a, s0, c):
    q, k, v, log_a = (to_chunks(t, c) for t in (q, k, v, log_a))
    b = jnp.cumsum(log_a, axis=3)
    b_last = b[:, :, :, -1:, :]
    q_in = q * jnp.exp(b)
    k_in = k * jnp.exp(-b)
    k_out = k * jnp.exp(b_last - b)
    causal = jnp.tril(jnp.ones((c, c), dtype=bool))
    a_intra = jnp.where(causal, jnp.einsum('bhnid,bhnjd->bhnij', q_in, k_in), 0.0)
    o_intra = jnp.einsum('bhnij,bhnjv->bhniv', a_intra, v)

    def step(s, inp):
        q_c, k_c, v_c, dec = inp
        o = jnp.einsum('bhid,bhdv->bhiv', q_c, s)
        s = s * dec[..., None] + jnp.einsum('bhid,bhiv->bhdv', k_c, v_c)
        return s, o

    xs = (chunk_first(q_in), chunk_first(k_out), chunk_first(v), chunk_first(jnp.exp(b_last[:, :, :, 0])))
    s_final, o_inter = lax.scan(step, s0, xs)
    o = o_intra + jnp.moveaxis(o_inter, 0, 2)
    return from_chunks(o), s_final


def gdn_chunk_scan(q, k, v, beta, log_a, s0, c):
    dv = v.shape[-1]
    q, k, v = (to_chunks(t, c) for t in (q, k, v))
    beta, log_a = (to_chunks(t[..., None], c)[..., 0] for t in (beta, log_a))
    g = jnp.cumsum(log_a, axis=-1)
    g_last = g[..., -1:]
    causal = jnp.tril(jnp.ones((c, c), dtype=bool))
    strict = jnp.tril(jnp.ones((c, c), dtype=bool), k=-1)
    diff = g[..., :, None] - g[..., None, :]
    decay_mat = jnp.exp(jnp.where(causal, diff, -jnp.inf))
    kb = k * beta[..., None]
    m = jnp.where(strict, jnp.einsum('bhnid,bhnjd->bhnij', kb, k) * decay_mat, 0.0)
    rhs = jnp.concatenate([v * beta[..., None], kb * jnp.exp(g)[..., None]], axis=-1)
    sol = lax.linalg.triangular_solve(m, rhs, left_side=True, lower=True, unit_diagonal=True)
    u, w = sol[..., :dv], sol[..., dv:]
    a_qk = jnp.einsum('bhnid,bhnjd->bhnij', q, k) * decay_mat
    q_dec = q * jnp.exp(g)[..., None]
    k_dec = k * jnp.exp(g_last - g)[..., None]

    def step(s, inp):
        q_c, k_c, u_c, w_c, a_c, tot = inp
        v_new = u_c - jnp.einsum('bhid,bhdv->bhiv', w_c, s)
        o = jnp.einsum('bhid,bhdv->bhiv', q_c, s) + jnp.einsum('bhij,bhjv->bhiv', a_c, v_new)
        s = s * tot[..., None, None] + jnp.einsum('bhid,bhiv->bhdv', k_c, v_new)
        return s, o

    xs = (chunk_first(q_dec), chunk_first(k_dec), chunk_first(u), chunk_first(w),
          chunk_first(a_qk), chunk_first(jnp.exp(g_last[..., 0])))
    s_final, o = lax.scan(step, s0, xs)
    return from_chunks(jnp.moveaxis(o, 0, 2)), s_final


def run_segments(scan_fn, s0, *args):
    meta_out, s = scan_fn(*(a[:, :N_META] for a in args), s0, N_META)
    real_out, _ = scan_fn(*(a[:, N_META:] for a in args), s, CHUNK)
    return jnp.concatenate([meta_out, real_out], axis=1)


def gla_mixer(h, w_in, w_gk, b_gk, head_norm, w_out):
    b, l, _ = h.shape
    f32 = jnp.float32
    proj = h @ w_in
    q, k, v, z, gk_low = jnp.split(
        proj, [GLA_QK, 2 * GLA_QK, 2 * GLA_QK + GLA_V, 2 * GLA_QK + 2 * GLA_V], axis=-1)
    gk = gk_low @ w_gk + b_gk
    log_a = (jax.nn.log_sigmoid(gk.astype(f32)) / GLA_GATE_NORMALIZER).reshape(b, l, GLA_HEADS, GLA_DK)
    q = q.astype(f32).reshape(b, l, GLA_HEADS, GLA_DK) * (GLA_DK ** -0.5)
    k = k.astype(f32).reshape(b, l, GLA_HEADS, GLA_DK)
    v = v.astype(f32).reshape(b, l, GLA_HEADS, GLA_DV)
    s0 = jnp.zeros((b, GLA_HEADS, GLA_DK, GLA_DV), f32)
    o = run_segments(gla_chunk_scan, s0, q, k, v, log_a)
    o = rms_norm(o, head_norm) * jax.nn.silu(z.astype(f32)).reshape(b, l, GLA_HEADS, GLA_DV)
    return o.reshape(b, l, GLA_V).astype(h.dtype) @ w_out


def gdn_mixer(h, w_in, conv_w, a_log, dt_bias, head_norm, w_out):
    b, l, _ = h.shape
    f32 = jnp.float32
    proj = h @ w_in
    qkv, z, beta_logit, a_logit = jnp.split(
        proj, [2 * GDN_QK + GDN_V, 2 * GDN_QK + 2 * GDN_V, 2 * GDN_QK + 2 * GDN_V + GDN_HEADS], axis=-1)
    qkv = jax.nn.silu(causal_depthwise_conv(qkv, conv_w)).astype(f32)
    q, k, v = jnp.split(qkv, [GDN_QK, 2 * GDN_QK], axis=-1)
    q = l2_norm(q.reshape(b, l, GDN_HEADS, GDN_DK)) * (GDN_DK ** -0.5)
    k = l2_norm(k.reshape(b, l, GDN_HEADS, GDN_DK))
    v = v.reshape(b, l, GDN_HEADS, GDN_DV)
    beta = jax.nn.sigmoid(beta_logit.astype(f32))
    log_a = -jnp.exp(a_log.astype(f32)) * jax.nn.softplus(a_logit.astype(f32) + dt_bias.astype(f32))
    s0 = jnp.zeros((b, GDN_HEADS, GDN_DK, GDN_DV), f32)
    o = run_segments(gdn_chunk_scan, s0, q, k, v, beta, log_a)
    o = rms_norm(o, head_norm) * jax.nn.silu(z.astype(f32)).reshape(b, l, GDN_HEADS, GDN_DV)
    return o.reshape(b, l, GDN_V).astype(h.dtype) @ w_out


def setup_inputs(seed: int = 0) -> dict:
    key = jax.random.key(seed)
    ks = jax.random.split(key, 20)
    f32 = jnp.float32

    def nrm(k, shape, scale):
        return jax.random.normal(k, shape, f32) * scale

    x = nrm(ks[0], (BATCH, SEQ, D_MODEL), 1.0)
    meta_tokens = nrm(ks[1], (N_META, D_MODEL), 1.0)
    gla_norm = 1.0 + nrm(ks[2], (N_GLA, D_MODEL), 0.02)
    gla_w_in = nrm(ks[3], (N_GLA, D_MODEL, GLA_IN), D_MODEL ** -0.5)
    gla_w_gk = nrm(ks[4], (N_GLA, GLA_GATE_RANK, GLA_QK), GLA_GATE_RANK ** -0.5)
    gla_b_gk = nrm(ks[5], (N_GLA, GLA_QK), 0.02)
    gla_head_norm = 1.0 + nrm(ks[6], (N_GLA, GLA_DV), 0.02)
    gla_w_out = nrm(ks[7], (N_GLA, GLA_V, D_MODEL), GLA_V ** -0.5)
    gdn_norm = 1.0 + nrm(ks[8], (N_GDN, D_MODEL), 0.02)
    gdn_w_in = nrm(ks[9], (N_GDN, D_MODEL, GDN_IN), D_MODEL ** -0.5)
    gdn_conv_w = nrm(ks[10], (N_GDN, CONV_WIDTH, 2 * GDN_QK + GDN_V), CONV_WIDTH ** -0.5)
    a_init = jax.random.uniform(ks[11], (N_GDN, GDN_HEADS), f32, 1.0, 16.0)
    gdn_a_log = jnp.log(a_init)
    dt = jnp.exp(jax.random.uniform(ks[12], (N_GDN, GDN_HEADS), f32, math.log(1e-3), math.log(1e-1)))
    gdn_dt_bias = dt + jnp.log(-jnp.expm1(-dt))
    gdn_head_norm = 1.0 + nrm(ks[13], (N_GDN, GDN_DV), 0.02)
    gdn_w_out = nrm(ks[14], (N_GDN, GDN_V, D_MODEL), GDN_V ** -0.5)
    final_norm = 1.0 + nrm(ks[15], (D_MODEL,), 0.02)
    return {'x': x, 'meta_tokens': meta_tokens,
            'gla_norm': gla_norm, 'gla_w_in': gla_w_in, 'gla_w_gk': gla_w_gk, 'gla_b_gk': gla_b_gk,
            'gla_head_norm': gla_head_norm, 'gla_w_out': gla_w_out,
            'gdn_norm': gdn_norm, 'gdn_w_in': gdn_w_in, 'gdn_conv_w': gdn_conv_w, 'gdn_a_log': gdn_a_log,
            'gdn_dt_bias': gdn_dt_bias, 'gdn_head_norm': gdn_head_norm, 'gdn_w_out': gdn_w_out,
            'final_norm': final_norm}


def reference(x, meta_tokens, gla_norm, gla_w_in, gla_w_gk, gla_b_gk, gla_head_norm, gla_w_out,
              gdn_norm, gdn_w_in, gdn_conv_w, gdn_a_log, gdn_dt_bias, gdn_head_norm, gdn_w_out,
              final_norm):
    b = x.shape[0]
    meta = jnp.broadcast_to(meta_tokens[None], (b, N_META, D_MODEL)).astype(x.dtype)
    h = jnp.concatenate([meta, x], axis=1)
    for i in range(DEPTH):
        j = i // 2
        if i % 2 == 0:
            h = h + gla_mixer(rms_norm(h, gla_norm[j]), gla_w_in[j], gla_w_gk[j], gla_b_gk[j],
                              gla_head_norm[j], gla_w_out[j])
        else:
            h = h + gdn_mixer(rms_norm(h, gdn_norm[j]), gdn_w_in[j], gdn_conv_w[j], gdn_a_log[j],
                              gdn_dt_bias[j], gdn_head_norm[j], gdn_w_out[j])
    h = rms_norm(h, final_norm)
    return h[:, N_META:]
```

```python
import functools

import jax
import jax.numpy as jnp
from jax import lax
from jax.experimental import pallas as pl
from jax.experimental.pallas import tpu as pltpu

F32 = jnp.float32
BF16 = jnp.bfloat16

N_META = 16
CHUNK = 64
NORM_EPS = 1e-6
GLA_HEADS = 4
GLA_DK = 128
GLA_DV = 256
GLA_GATE_RANK = 16
GLA_GATE_NORMALIZER = 16.0
GDN_HEADS = 8
GDN_DK = 128
GDN_DV = 256
CONV_WIDTH = 4

LANES = 128
SUBLANES = 8
TBLK = 256
LEAD = TBLK
ROW_TILE = 512
COL_TILE = 512
VMEM_LIMIT = 56 * 1024 * 1024
GDN_HEADS_PER_STEP = 2
NEUMANN_DOUBLINGS = 5


def _dot(a, b):
    return jnp.dot(a, b, preferred_element_type=F32)


def _dot_nt(a, b):
    return lax.dot_general(a, b, (((1,), (1,)), ((), ())), preferred_element_type=F32)


def _dot_tn(a, b):
    return lax.dot_general(a, b, (((0,), (0,)), ((), ())), preferred_element_type=F32)


def _softplus(x):
    return jnp.maximum(x, 0.0) + jnp.log1p(jnp.exp(-jnp.abs(x)))


def _silu(x):
    return x * jax.nn.sigmoid(x)


def _chunk_masks(n):
    row = lax.broadcasted_iota(jnp.int32, (n, n), 0)
    col = lax.broadcasted_iota(jnp.int32, (n, n), 1)
    same = (row // CHUNK) == (col // CHUNK)
    return same & (col <= row), same & (col < row)


def _chunk_cumsum(tri, x):
    x1 = x.astype(BF16)
    r = x - x1.astype(F32)
    x2 = r.astype(BF16)
    r = r - x2.astype(F32)
    x3 = r.astype(BF16)
    return _dot(tri, x1) + _dot(tri, x2) + _dot(tri, x3)


def _chunk_last(x):
    n = x.shape[0]
    parts = []
    for c in range(n // CHUNK):
        last = x[c * CHUNK + CHUNK - 1:c * CHUNK + CHUNK, :]
        parts.append(jnp.broadcast_to(last, (CHUNK, x.shape[1])))
    return jnp.concatenate(parts, axis=0)


def _lane_column(x, idx):
    lane = lax.broadcasted_iota(jnp.int32, x.shape, 1)
    return jnp.sum(jnp.where(lane == idx, x, 0.0), axis=1, keepdims=True)


def _valid_rows(t, n):
    rid = t * n + lax.broadcasted_iota(jnp.int32, (n, 1), 0)
    return rid >= (LEAD - N_META)


def _head_norm_gate(o, z, w):
    ms = jnp.mean(o * o, axis=-1, keepdims=True)
    return (o * lax.rsqrt(ms + NORM_EPS) * w) * _silu(z)


def _inproj_kernel(h_ref, nw_ref, wm_ref, wt_ref, om_ref, ot_ref):
    h = h_ref[...]
    ms = jnp.mean(h * h, axis=-1, keepdims=True)
    hn = (h * lax.rsqrt(ms + NORM_EPS) * nw_ref[...]).astype(BF16)
    for j in range(wm_ref.shape[1] // COL_TILE):
        cols = slice(j * COL_TILE, (j + 1) * COL_TILE)
        om_ref[:, cols] = _dot(hn, wm_ref[:, cols]).astype(BF16)
    ot_ref[...] = _dot(hn, wt_ref[...])


def _inproj(h2, norm_w, w_main, w_tail):
    m, d = h2.shape
    n_main = w_main.shape[1]
    return pl.pallas_call(
        _inproj_kernel,
        grid=(m // ROW_TILE,),
        in_specs=[
            pl.BlockSpec((ROW_TILE, d), lambda i: (i, 0)),
            pl.BlockSpec((1, d), lambda i: (0, 0)),
            pl.BlockSpec((d, n_main), lambda i: (0, 0)),
            pl.BlockSpec((d, LANES), lambda i: (0, 0)),
        ],
        out_specs=[
            pl.BlockSpec((ROW_TILE, n_main), lambda i: (i, 0)),
            pl.BlockSpec((ROW_TILE, LANES), lambda i: (i, 0)),
        ],
        out_shape=[
            jax.ShapeDtypeStruct((m, n_main), BF16),
            jax.ShapeDtypeStruct((m, LANES), F32),
        ],
        compiler_params=pltpu.CompilerParams(
            dimension_semantics=("arbitrary",), vmem_limit_bytes=VMEM_LIMIT),
        name="inproj",
    )(h2, norm_w, w_main, w_tail)


def _outproj_kernel(o_ref, h_ref, w_ref, out_ref):
    out_ref[...] = h_ref[...] + _dot(o_ref[...], w_ref[...])


def _outproj_final_kernel(o_ref, h_ref, w_ref, nw_ref, out_ref):
    h = h_ref[...] + _dot(o_ref[...], w_ref[...])
    ms = jnp.mean(h * h, axis=-1, keepdims=True)
    out_ref[...] = h * lax.rsqrt(ms + NORM_EPS) * nw_ref[...]


def _outproj(o2, h2, w_out, final_norm=None):
    m, d = h2.shape
    v = o2.shape[1]
    in_specs = [
        pl.BlockSpec((ROW_TILE, v), lambda i: (i, 0)),
        pl.BlockSpec((ROW_TILE, d), lambda i: (i, 0)),
        pl.BlockSpec((v, d), lambda i: (0, 0)),
    ]
    args = [o2, h2, w_out]
    body = _outproj_kernel
    if final_norm is not None:
        in_specs.append(pl.BlockSpec((1, d), lambda i: (0, 0)))
        args.append(final_norm)
        body = _outproj_final_kernel
    return pl.pallas_call(
        body,
        grid=(m // ROW_TILE,),
        in_specs=in_specs,
        out_specs=pl.BlockSpec((ROW_TILE, d), lambda i: (i, 0)),
        out_shape=jax.ShapeDtypeStruct((m, d), F32),
        compiler_params=pltpu.CompilerParams(
            dimension_semantics=("arbitrary",), vmem_limit_bytes=VMEM_LIMIT),
        name="outproj",
    )(*args)


def _gla_scan_kernel(q_ref, k_ref, v_ref, z_ref, tail_ref, wgk_ref, bgk_ref, hn_ref,
                     o_ref, st_ref):
    t = pl.program_id(1)

    @pl.when(t == 0)
    def _():
        st_ref[...] = jnp.zeros_like(st_ref)

    causal, _ = _chunk_masks(TBLK)
    tri = jnp.where(causal, 1.0, 0.0).astype(BF16)
    causal_c = causal[:CHUNK, :CHUNK]

    gk = _dot(tail_ref[0].astype(BF16), wgk_ref[...]) + bgk_ref[...]
    log_a = -_softplus(-gk) / GLA_GATE_NORMALIZER
    log_a = jnp.where(_valid_rows(t, TBLK), log_a, 0.0)
    b = _chunk_cumsum(tri, log_a)
    b_last = _chunk_last(b)
    hn_w = hn_ref[...]

    for c in range(TBLK // CHUNK):
        rows = slice(c * CHUNK, (c + 1) * CHUNK)
        for h in range(GLA_HEADS):
            kcols = slice(h * GLA_DK, (h + 1) * GLA_DK)
            vcols = slice(h * GLA_DV, (h + 1) * GLA_DV)
            bc = b[rows, kcols]
            blc = b_last[rows, kcols]
            qc = q_ref[0, rows, kcols].astype(F32) * (GLA_DK ** -0.5)
            kc = k_ref[0, rows, kcols].astype(F32)
            vc = v_ref[0, rows, vcols]
            q_in = (qc * jnp.exp(bc)).astype(BF16)
            k_in = (kc * jnp.exp(-bc)).astype(BF16)
            k_out = (kc * jnp.exp(blc - bc)).astype(BF16)
            a = jnp.where(causal_c, _dot_nt(q_in, k_in), 0.0).astype(BF16)
            st = st_ref[h]
            o = _dot(a, vc) + _dot_nt(q_in, st.astype(BF16))
            st_ref[h] = st * jnp.exp(blc[0:1, :]) + _dot_tn(vc, k_out)
            zc = z_ref[0, rows, vcols].astype(F32)
            o_ref[0, rows, vcols] = _head_norm_gate(o, zc, hn_w).astype(BF16)


def _gla_scan(main, tail, w_gk, b_gk, head_norm):
    bsz, lp, _ = main.shape
    qk = GLA_HEADS * GLA_DK
    vv = GLA_HEADS * GLA_DV
    return pl.pallas_call(
        _gla_scan_kernel,
        grid=(bsz, lp // TBLK),
        in_specs=[
            pl.BlockSpec((1, TBLK, qk), lambda b, t: (b, t, 0)),
            pl.BlockSpec((1, TBLK, qk), lambda b, t: (b, t, 1)),
            pl.BlockSpec((1, TBLK, vv), lambda b, t: (b, t, 1)),
            pl.BlockSpec((1, TBLK, vv), lambda b, t: (b, t, 2)),
            pl.BlockSpec((1, TBLK, LANES), lambda b, t: (b, t, 0)),
            pl.BlockSpec((LANES, qk), lambda b, t: (0, 0)),
            pl.BlockSpec((1, qk), lambda b, t: (0, 0)),
            pl.BlockSpec((1, GLA_DV), lambda b, t: (0, 0)),
        ],
        out_specs=pl.BlockSpec((1, TBLK, vv), lambda b, t: (b, t, 0)),
        out_shape=jax.ShapeDtypeStruct((bsz, lp, vv), BF16),
        scratch_shapes=[pltpu.VMEM((GLA_HEADS, GLA_DV, GLA_DK), F32)],
        compiler_params=pltpu.CompilerParams(
            dimension_semantics=("arbitrary", "arbitrary"), vmem_limit_bytes=VMEM_LIMIT),
        name="gla_scan",
    )(main, main, main, main, tail, w_gk, b_gk, head_norm)


def _gdn_scan_kernel(q_ref, k_ref, v_ref, z_ref, tail_ref, cwq_ref, cwk_ref, cwv_ref,
                     alog_ref, dtb_ref, hn_ref, o_ref,
                     st_ref, cq_ref, ck_ref, cv_ref, gt_ref, *, heads_per_step):
    t = pl.program_id(1)
    hg = pl.program_id(2)

    @pl.when(t == 0)
    def _():
        st_ref[pl.ds(hg * heads_per_step, heads_per_step)] = jnp.zeros(
            (heads_per_step,) + st_ref.shape[1:], F32)
        cq_ref[hg] = jnp.zeros(cq_ref.shape[1:], F32)
        ck_ref[hg] = jnp.zeros(ck_ref.shape[1:], F32)
        cv_ref[hg] = jnp.zeros(cv_ref.shape[1:], F32)

    causal, strict = _chunk_masks(TBLK)
    tri = jnp.where(causal, 1.0, 0.0).astype(BF16)

    tl = tail_ref[0]
    beta_all = jax.nn.sigmoid(tl)
    la_all = -jnp.exp(alog_ref[...]) * _softplus(tl + dtb_ref[...])
    la_all = jnp.where(_valid_rows(t, TBLK), la_all, 0.0)
    g_all = _chunk_cumsum(tri, la_all)
    gl_all = _chunk_last(g_all)
    gt_ref[...] = g_all.T
    hn_w = hn_ref[...]

    def conv_silu(x_ref, carry_ref, w_ref):
        x = x_ref[0].astype(F32)
        ext = jnp.concatenate([carry_ref[hg], x], axis=0)
        carry_ref[hg] = x[TBLK - SUBLANES:, :]
        y = None
        for j in range(CONV_WIDTH):
            s = SUBLANES - (CONV_WIDTH - 1) + j
            term = w_ref[j:j + 1, :] * ext[s:s + TBLK, :]
            y = term if y is None else y + term
        return _silu(y)

    q_all = conv_silu(q_ref, cq_ref, cwq_ref)
    k_all = conv_silu(k_ref, ck_ref, cwk_ref)
    v_all = conv_silu(v_ref, cv_ref, cwv_ref)

    for i in range(heads_per_step):
        hidx = hg * heads_per_step + i
        kcols = slice(i * GDN_DK, (i + 1) * GDN_DK)
        vcols = slice(i * GDN_DV, (i + 1) * GDN_DV)
        beta = _lane_column(beta_all, hidx)
        g = _lane_column(g_all, GDN_HEADS + hidx)
        gl = _lane_column(gl_all, GDN_HEADS + hidx)
        g_row = gt_ref[pl.ds(GDN_HEADS + hidx, 1), :]
        eg = jnp.exp(g)
        egl = jnp.exp(gl - g)

        q = q_all[:, kcols]
        k = k_all[:, kcols]
        v = v_all[:, vcols]
        q = q * lax.rsqrt(jnp.sum(q * q, axis=-1, keepdims=True) + NORM_EPS) * (GDN_DK ** -0.5)
        k = k * lax.rsqrt(jnp.sum(k * k, axis=-1, keepdims=True) + NORM_EPS)
        kb = k * beta
        k_b16 = k.astype(BF16)

        decay = jnp.exp(jnp.where(causal, g - g_row, -jnp.inf))
        kk = _dot_nt(jnp.concatenate([kb, q], axis=0).astype(BF16), k_b16)
        m = jnp.where(strict, kk[:TBLK] * decay, 0.0)
        a_qk = (kk[TBLK:] * decay).astype(BF16)

        x = -m
        qm = x
        for _ in range(NEUMANN_DOUBLINGS):
            xb = x.astype(BF16)
            x = _dot(xb, xb)
            qm = qm + x + _dot(qm.astype(BF16), x.astype(BF16))
        rhs = jnp.concatenate([v * beta, kb * eg], axis=1)
        sol = rhs + _dot(qm.astype(BF16), rhs.astype(BF16))
        u = sol[:, :GDN_DV]
        w = sol[:, GDN_DV:]
        q_dec = q * eg
        k_dec = (k * egl).astype(BF16)

        outs = []
        for c in range(TBLK // CHUNK):
            rows = slice(c * CHUNK, (c + 1) * CHUNK)
            st = st_ref[hidx]
            lhs = jnp.concatenate([w[rows], q_dec[rows]], axis=0).astype(BF16)
            r = _dot_nt(lhs, st.astype(BF16))
            v_new = (u[rows] - r[:CHUNK]).astype(BF16)
            outs.append(r[CHUNK:] + _dot(a_qk[rows, rows], v_new))
            tot = jnp.exp(gl[c * CHUNK:c * CHUNK + 1, :])
            st_ref[hidx] = st * tot + _dot_tn(v_new, k_dec[rows])
        o = jnp.concatenate(outs, axis=0)
        zc = z_ref[0, :, vcols].astype(F32)
        o_ref[0, :, vcols] = _head_norm_gate(o, zc, hn_w).astype(BF16)


def _gdn_scan(main, tail, conv_w, alog_row, dtb_row, head_norm):
    bsz, lp, _ = main.shape
    hps = GDN_HEADS_PER_STEP
    groups = GDN_HEADS // hps
    qk = GDN_HEADS * GDN_DK
    kw = hps * GDN_DK
    vw = hps * GDN_DV
    k_off = qk // kw
    v_off = 2 * qk // vw
    z_off = (2 * qk + GDN_HEADS * GDN_DV) // vw
    kernel = functools.partial(_gdn_scan_kernel, heads_per_step=hps)
    return pl.pallas_call(
        kernel,
        grid=(bsz, lp // TBLK, groups),
        in_specs=[
            pl.BlockSpec((1, TBLK, kw), lambda b, t, g: (b, t, g)),
            pl.BlockSpec((1, TBLK, kw), lambda b, t, g: (b, t, k_off + g)),
            pl.BlockSpec((1, TBLK, vw), lambda b, t, g: (b, t, v_off + g)),
            pl.BlockSpec((1, TBLK, vw), lambda b, t, g: (b, t, z_off + g)),
            pl.BlockSpec((1, TBLK, LANES), lambda b, t, g: (b, t, 0)),
            pl.BlockSpec((CONV_WIDTH, kw), lambda b, t, g: (0, g)),
            pl.BlockSpec((CONV_WIDTH, kw), lambda b, t, g: (0, k_off + g)),
            pl.BlockSpec((CONV_WIDTH, vw), lambda b, t, g: (0, v_off + g)),
            pl.BlockSpec((1, LANES), lambda b, t, g: (0, 0)),
            pl.BlockSpec((1, LANES), lambda b, t, g: (0, 0)),
            pl.BlockSpec((1, GDN_DV), lambda b, t, g: (0, 0)),
        ],
        out_specs=pl.BlockSpec((1, TBLK, vw), lambda b, t, g: (b, t, g)),
        out_shape=jax.ShapeDtypeStruct((bsz, lp, GDN_HEADS * GDN_DV), BF16),
        scratch_shapes=[
            pltpu.VMEM((GDN_HEADS, GDN_DV, GDN_DK), F32),
            pltpu.VMEM((groups, SUBLANES, kw), F32),
            pltpu.VMEM((groups, SUBLANES, kw), F32),
            pltpu.VMEM((groups, SUBLANES, vw), F32),
            pltpu.VMEM((LANES, TBLK), F32),
        ],
        compiler_params=pltpu.CompilerParams(
            dimension_semantics=("arbitrary", "arbitrary", "arbitrary"),
            vmem_limit_bytes=VMEM_LIMIT),
        name="gdn_scan",
    )(main, main, main, main, tail, conv_w, conv_w, conv_w, alog_row, dtb_row, head_norm)


def _split_w_in(w_in, n_main):
    tail = w_in[:, n_main:]
    tail = jnp.pad(tail, ((0, 0), (0, LANES - tail.shape[1])))
    return w_in[:, :n_main].astype(BF16), tail.astype(BF16)


def _lane_row(x, offset):
    return jnp.pad(x.astype(F32), (offset, LANES - offset - x.shape[0]))[None, :]


def kernel(x, meta_tokens, gla_norm, gla_w_in, gla_w_gk, gla_b_gk, gla_head_norm, gla_w_out,
           gdn_norm, gdn_w_in, gdn_conv_w, gdn_a_log, gdn_dt_bias, gdn_head_norm, gdn_w_out,
           final_norm):
    bsz, seq, d = x.shape
    assert seq % TBLK == 0 and (bsz * (LEAD + seq)) % ROW_TILE == 0
    lp = LEAD + seq
    lead = jnp.concatenate(
        [jnp.zeros((LEAD - N_META, d), x.dtype), meta_tokens.astype(x.dtype)], axis=0)
    h = jnp.concatenate([jnp.broadcast_to(lead[None], (bsz, LEAD, d)), x], axis=1)
    h2 = h.reshape(bsz * lp, d)

    depth = gla_w_in.shape[0] + gdn_w_in.shape[0]
    gla_main = 2 * GLA_HEADS * GLA_DK + 2 * GLA_HEADS * GLA_DV
    gdn_main = 2 * GDN_HEADS * GDN_DK + 2 * GDN_HEADS * GDN_DV
    for i in range(depth):
        j = i // 2
        fin = final_norm[None, :] if i == depth - 1 else None
        if i % 2 == 0:
            w_main, w_tail = _split_w_in(gla_w_in[j], gla_main)
            main, tail = _inproj(h2, gla_norm[j][None, :], w_main, w_tail)
            w_gk = jnp.pad(gla_w_gk[j], ((0, LANES - GLA_GATE_RANK), (0, 0))).astype(BF16)
            og = _gla_scan(main.reshape(bsz, lp, gla_main), tail.reshape(bsz, lp, LANES),
                           w_gk, gla_b_gk[j][None, :], gla_head_norm[j][None, :])
            h2 = _outproj(og.reshape(bsz * lp, -1), h2, gla_w_out[j].astype(BF16), fin)
        else:
            w_main, w_tail = _split_w_in(gdn_w_in[j], gdn_main)
            main, tail = _inproj(h2, gdn_norm[j][None, :], w_main, w_tail)
            og = _gdn_scan(main.reshape(bsz, lp, gdn_main), tail.reshape(bsz, lp, LANES),
                           gdn_conv_w[j], _lane_row(gdn_a_log[j], GDN_HEADS),
                           _lane_row(gdn_dt_bias[j], GDN_HEADS), gdn_head_norm[j][None, :])
            h2 = _outproj(og.reshape(bsz * lp, -1), h2, gdn_w_out[j].astype(BF16), fin)
    return h2.reshape(bsz, lp, d)[:, LEAD:]
```

```python
import jax
import jax.numpy as jnp
from jax import lax
from jax.experimental import pallas as pl
from jax.experimental.pallas import tpu as pltpu

F32 = jnp.float32
BF16 = jnp.bfloat16

N_META = 16
CHUNK = 64
NORM_EPS = 1e-6
GLA_HEADS = 4
GLA_DK = 128
GLA_DV = 256
GLA_GATE_RANK = 16
GLA_GATE_NORMALIZER = 16.0
GDN_HEADS = 8
GDN_DK = 128
GDN_DV = 256
CONV_WIDTH = 4

LANES = 128
SUBLANES = 8
TBLK = 256
NCHUNK = TBLK // CHUNK
LEAD = TBLK
ROW_TILE = 512
COL_TILE = 512
VMEM_LIMIT = 56 * 1024 * 1024
NEUMANN_DOUBLINGS = 5


def _dot(a, b):
    return jnp.dot(a, b, preferred_element_type=F32)


def _dot_nt(a, b):
    return lax.dot_general(a, b, (((1,), (1,)), ((), ())), preferred_element_type=F32)


def _dot_tn(a, b):
    return lax.dot_general(a, b, (((0,), (0,)), ((), ())), preferred_element_type=F32)


def _softplus(x):
    return jnp.maximum(x, 0.0) + jnp.log1p(jnp.exp(-jnp.abs(x)))


def _silu(x):
    return x * jax.nn.sigmoid(x)


def _chunk_masks(n):
    row = lax.broadcasted_iota(jnp.int32, (n, n), 0)
    col = lax.broadcasted_iota(jnp.int32, (n, n), 1)
    same = (row // CHUNK) == (col // CHUNK)
    return same & (col <= row), same & (col < row)


def _chunk_cumsum(tri, x):
    x1 = x.astype(BF16)
    r = x - x1.astype(F32)
    x2 = r.astype(BF16)
    r = r - x2.astype(F32)
    x3 = r.astype(BF16)
    return _dot(tri, x1) + _dot(tri, x2) + _dot(tri, x3)


def _chunk_last(x):
    n = x.shape[0]
    parts = []
    for c in range(n // CHUNK):
        last = x[c * CHUNK + CHUNK - 1:c * CHUNK + CHUNK, :]
        parts.append(jnp.broadcast_to(last, (CHUNK, x.shape[1])))
    return jnp.concatenate(parts, axis=0)


def _lane_column(x, idx):
    lane = lax.broadcasted_iota(jnp.int32, x.shape, 1)
    return jnp.sum(jnp.where(lane == idx, x, 0.0), axis=1, keepdims=True)


def _valid_rows(t, n):
    rid = t * n + lax.broadcasted_iota(jnp.int32, (n, 1), 0)
    return rid >= (LEAD - N_META)


def _head_norm_gate(o, z, w):
    ms = jnp.mean(o * o, axis=-1, keepdims=True)
    return (o * lax.rsqrt(ms + NORM_EPS) * w) * _silu(z)


def _rms_to_bf16(h, w):
    ms = jnp.mean(h * h, axis=-1, keepdims=True)
    return (h * lax.rsqrt(ms + NORM_EPS) * w).astype(BF16)


def _inproj_kernel(h_ref, nw_ref, wm_ref, wt_ref, om_ref, ot_ref):
    hn = _rms_to_bf16(h_ref[...], nw_ref[...])
    for j in range(wm_ref.shape[1] // COL_TILE):
        cols = slice(j * COL_TILE, (j + 1) * COL_TILE)
        om_ref[:, cols] = _dot(hn, wm_ref[:, cols]).astype(BF16)
    ot_ref[...] = _dot(hn, wt_ref[...])


def _inproj(h2, norm_w, w_main, w_tail):
    m, d = h2.shape
    n_main = w_main.shape[1]
    return pl.pallas_call(
        _inproj_kernel,
        grid=(m // ROW_TILE,),
        in_specs=[
            pl.BlockSpec((ROW_TILE, d), lambda i: (i, 0)),
            pl.BlockSpec((1, d), lambda i: (0, 0)),
            pl.BlockSpec((d, n_main), lambda i: (0, 0)),
            pl.BlockSpec((d, LANES), lambda i: (0, 0)),
        ],
        out_specs=[
            pl.BlockSpec((ROW_TILE, n_main), lambda i: (i, 0)),
            pl.BlockSpec((ROW_TILE, LANES), lambda i: (i, 0)),
        ],
        out_shape=[
            jax.ShapeDtypeStruct((m, n_main), BF16),
            jax.ShapeDtypeStruct((m, LANES), F32),
        ],
        compiler_params=pltpu.CompilerParams(
            dimension_semantics=("arbitrary",), vmem_limit_bytes=VMEM_LIMIT),
        name="inproj",
    )(h2, norm_w, w_main, w_tail)


def _gdn_inproj_kernel(h_ref, nw_ref, wm_ref, wt_ref, cw_ref, om_ref, ot_ref, carry_ref):
    t = pl.program_id(1)

    @pl.when(t == 0)
    def _():
        carry_ref[...] = jnp.zeros_like(carry_ref)

    hn = _rms_to_bf16(h_ref[0], nw_ref[...])
    n_conv = cw_ref.shape[1]
    n_qk = 2 * GDN_HEADS * GDN_DK
    for j in range(wm_ref.shape[1] // COL_TILE):
        c0 = j * COL_TILE
        cols = slice(c0, c0 + COL_TILE)
        p = _dot(hn, wm_ref[:, cols])
        if c0 >= n_conv:
            om_ref[0, :, cols] = p.astype(BF16)
            continue
        ext = jnp.concatenate([carry_ref[:, cols], p], axis=0)
        carry_ref[:, cols] = p[TBLK - SUBLANES:, :]
        x1 = pltpu.roll(ext, 1, axis=0)
        near = cw_ref[3:4, cols] * ext + cw_ref[2:3, cols] * x1
        far = cw_ref[1:2, cols] * ext + cw_ref[0:1, cols] * x1
        y = _silu((near + pltpu.roll(far, 2, axis=0))[SUBLANES:, :])
        if c0 < n_qk:
            scale = GDN_DK ** -0.5 if c0 < n_qk // 2 else None
            parts = []
            for i in range(COL_TILE // GDN_DK):
                yh = y[:, i * GDN_DK:(i + 1) * GDN_DK]
                yh = yh * lax.rsqrt(jnp.sum(yh * yh, axis=-1, keepdims=True) + NORM_EPS)
                parts.append(yh * scale if scale is not None else yh)
            y = jnp.concatenate(parts, axis=1)
        om_ref[0, :, cols] = y.astype(BF16)
    ot_ref[0] = _dot(hn, wt_ref[...])


def _gdn_inproj(h3, norm_w, w_main, w_tail, conv_w):
    bsz, lp, d = h3.shape
    n_main = w_main.shape[1]
    n_conv = conv_w.shape[1]
    return pl.pallas_call(
        _gdn_inproj_kernel,
        grid=(bsz, lp // TBLK),
        in_specs=[
            pl.BlockSpec((1, TBLK, d), lambda b, t: (b, t, 0)),
            pl.BlockSpec((1, d), lambda b, t: (0, 0)),
            pl.BlockSpec((d, n_main), lambda b, t: (0, 0)),
            pl.BlockSpec((d, LANES), lambda b, t: (0, 0)),
            pl.BlockSpec((CONV_WIDTH, n_conv), lambda b, t: (0, 0)),
        ],
        out_specs=[
            pl.BlockSpec((1, TBLK, n_main), lambda b, t: (b, t, 0)),
            pl.BlockSpec((1, TBLK, LANES), lambda b, t: (b, t, 0)),
        ],
        out_shape=[
            jax.ShapeDtypeStruct((bsz, lp, n_main), BF16),
            jax.ShapeDtypeStruct((bsz, lp, LANES), F32),
        ],
        scratch_shapes=[pltpu.VMEM((SUBLANES, n_conv), F32)],
        compiler_params=pltpu.CompilerParams(
            dimension_semantics=("arbitrary", "arbitrary"), vmem_limit_bytes=VMEM_LIMIT),
        name="gdn_inproj",
    )(h3, norm_w, w_main, w_tail, conv_w)


def _outproj_kernel(o_ref, h_ref, w_ref, out_ref):
    out_ref[...] = h_ref[...] + _dot(o_ref[...], w_ref[...])


def _outproj_final_kernel(o_ref, h_ref, w_ref, nw_ref, out_ref):
    h = h_ref[...] + _dot(o_ref[...], w_ref[...])
    ms = jnp.mean(h * h, axis=-1, keepdims=True)
    out_ref[...] = h * lax.rsqrt(ms + NORM_EPS) * nw_ref[...]


def _outproj(o2, h2, w_out, final_norm=None):
    m, d = h2.shape
    v = o2.shape[1]
    in_specs = [
        pl.BlockSpec((ROW_TILE, v), lambda i: (i, 0)),
        pl.BlockSpec((ROW_TILE, d), lambda i: (i, 0)),
        pl.BlockSpec((v, d), lambda i: (0, 0)),
    ]
    args = [o2, h2, w_out]
    body = _outproj_kernel
    if final_norm is not None:
        in_specs.append(pl.BlockSpec((1, d), lambda i: (0, 0)))
        args.append(final_norm)
        body = _outproj_final_kernel
    return pl.pallas_call(
        body,
        grid=(m // ROW_TILE,),
        in_specs=in_specs,
        out_specs=pl.BlockSpec((ROW_TILE, d), lambda i: (i, 0)),
        out_shape=jax.ShapeDtypeStruct((m, d), F32),
        compiler_params=pltpu.CompilerParams(
            dimension_semantics=("arbitrary",), vmem_limit_bytes=VMEM_LIMIT),
        name="outproj",
    )(*args)


def _gla_scan_kernel(q_ref, k_ref, v_ref, z_ref, tail_ref, wgk_ref, bgk_ref, hn_ref,
                     o_ref, st_ref):
    t = pl.program_id(1)

    @pl.when(t == 0)
    def _():
        st_ref[...] = jnp.zeros_like(st_ref)

    causal, _ = _chunk_masks(TBLK)
    tri = jnp.where(causal, 1.0, 0.0).astype(BF16)
    causal_c = causal[:CHUNK, :CHUNK]

    gk = _dot(tail_ref[0].astype(BF16), wgk_ref[...]) + bgk_ref[...]
    log_a = -_softplus(-gk) / GLA_GATE_NORMALIZER
    log_a = jnp.where(_valid_rows(t, TBLK), log_a, 0.0)
    b = _chunk_cumsum(tri, log_a)
    b_last = _chunk_last(b)
    hn_w = hn_ref[...]

    for c in range(NCHUNK):
        rows = slice(c * CHUNK, (c + 1) * CHUNK)
        for h in range(GLA_HEADS):
            kcols = slice(h * GLA_DK, (h + 1) * GLA_DK)
            vcols = slice(h * GLA_DV, (h + 1) * GLA_DV)
            bc = b[rows, kcols]
            blc = b_last[rows, kcols]
            qc = q_ref[0, rows, kcols].astype(F32) * (GLA_DK ** -0.5)
            kc = k_ref[0, rows, kcols].astype(F32)
            vc = v_ref[0, rows, vcols]
            q_in = (qc * jnp.exp(bc)).astype(BF16)
            k_in = (kc * jnp.exp(-bc)).astype(BF16)
            k_out = (kc * jnp.exp(blc - bc)).astype(BF16)
            a = jnp.where(causal_c, _dot_nt(q_in, k_in), 0.0).astype(BF16)
            st = st_ref[h]
            o = _dot(a, vc) + _dot_nt(q_in, st.astype(BF16))
            st_ref[h] = st * jnp.exp(blc[0:1, :]) + _dot_tn(vc, k_out)
            zc = z_ref[0, rows, vcols].astype(F32)
            o_ref[0, rows, vcols] = _head_norm_gate(o, zc, hn_w).astype(BF16)


def _gla_scan(main, tail, w_gk, b_gk, head_norm):
    bsz, lp, _ = main.shape
    qk = GLA_HEADS * GLA_DK
    vv = GLA_HEADS * GLA_DV
    return pl.pallas_call(
        _gla_scan_kernel,
        grid=(bsz, lp // TBLK),
        in_specs=[
            pl.BlockSpec((1, TBLK, qk), lambda b, t: (b, t, 0)),
            pl.BlockSpec((1, TBLK, qk), lambda b, t: (b, t, 1)),
            pl.BlockSpec((1, TBLK, vv), lambda b, t: (b, t, 1)),
            pl.BlockSpec((1, TBLK, vv), lambda b, t: (b, t, 2)),
            pl.BlockSpec((1, TBLK, LANES), lambda b, t: (b, t, 0)),
            pl.BlockSpec((LANES, qk), lambda b, t: (0, 0)),
            pl.BlockSpec((1, qk), lambda b, t: (0, 0)),
            pl.BlockSpec((1, GLA_DV), lambda b, t: (0, 0)),
        ],
        out_specs=pl.BlockSpec((1, TBLK, vv), lambda b, t: (b, t, 0)),
        out_shape=jax.ShapeDtypeStruct((bsz, lp, vv), BF16),
        scratch_shapes=[pltpu.VMEM((GLA_HEADS, GLA_DV, GLA_DK), F32)],
        compiler_params=pltpu.CompilerParams(
            dimension_semantics=("arbitrary", "arbitrary"), vmem_limit_bytes=VMEM_LIMIT),
        name="gla_scan",
    )(main, main, main, main, tail, w_gk, b_gk, head_norm)


def _gdn_scan_kernel(q_ref, k_ref, v_ref, z_ref, tail_ref, alog_ref, dtb_ref, hn_ref,
                     o_ref, s_ref, u_ref, lhs_ref, ak_ref):
    t = pl.program_id(1)

    @pl.when(t == 0)
    def _():
        s_ref[...] = jnp.zeros_like(s_ref)

    causal, _ = _chunk_masks(TBLK)
    tri = jnp.where(causal, 1.0, 0.0).astype(BF16)
    ci = lax.broadcasted_iota(jnp.int32, (CHUNK, TBLK), 0)
    lane = lax.broadcasted_iota(jnp.int32, (CHUNK, TBLK), 1)
    cj = lane % CHUNK
    cblk = lane // CHUNK
    in_blk = [cblk == c for c in range(NCHUNK)]
    le = cj <= ci
    lt = cj < ci

    def compact(blocks):
        out = blocks[NCHUNK - 1]
        for c in range(NCHUNK - 2, -1, -1):
            out = jnp.where(in_blk[c], blocks[c], out)
        return out

    def block_diag(xc):
        zero = jnp.zeros_like(xc)
        return jnp.concatenate([jnp.where(in_blk[c], xc, zero) for c in range(NCHUNK)], axis=0)

    def row_blocks(x):
        return [x[c * CHUNK:(c + 1) * CHUNK] for c in range(NCHUNK)]

    tl = tail_ref[0]
    beta_all = jax.nn.sigmoid(tl)
    la_all = -jnp.exp(alog_ref[...]) * _softplus(tl + dtb_ref[...])
    la_all = jnp.where(_valid_rows(t, TBLK), la_all, 0.0)
    g_all = _chunk_cumsum(tri, la_all)
    gl_all = _chunk_last(g_all)
    g_t = g_all.T
    hn_w = hn_ref[...]

    heads = range(GDN_HEADS)
    kcols = [slice(h * GDN_DK, (h + 1) * GDN_DK) for h in heads]
    vcols = [slice(h * GDN_DV, (h + 1) * GDN_DV) for h in heads]
    beta = [_lane_column(beta_all, h) for h in heads]
    g = [_lane_column(g_all, GDN_HEADS + h) for h in heads]
    gl = [_lane_column(gl_all, GDN_HEADS + h) for h in heads]
    eg = [jnp.exp(g[h]) for h in heads]

    x, a_c = [], []
    for h in heads:
        k_b16 = k_ref[0, :, kcols[h]]
        kb = k_b16.astype(F32) * beta[h]
        kq = jnp.concatenate([kb.astype(BF16), q_ref[0, :, kcols[h]]], axis=0)
        kk = _dot_nt(kq, k_b16)
        g_row = g_t[GDN_HEADS + h:GDN_HEADS + h + 1, :]
        g_c = compact(row_blocks(g[h]))
        decay = jnp.exp(jnp.where(le, g_c - g_row, -jnp.inf))
        x.append(-jnp.where(lt, compact(row_blocks(kk[:TBLK])) * decay, 0.0))
        a_c.append(compact(row_blocks(kk[TBLK:])) * decay)

    qm = list(x)
    for h in heads:
        xb = x[h].astype(BF16)
        x[h] = _dot(xb, block_diag(xb))
    for _ in range(NEUMANN_DOUBLINGS - 1):
        for h in heads:
            xb = x[h].astype(BF16)
            r = _dot(jnp.concatenate([qm[h].astype(BF16), xb], axis=0), block_diag(xb))
            qm[h] = qm[h] + x[h] + r[:CHUNK]
            x[h] = r[CHUNK:]
    for h in heads:
        qm[h] = qm[h] + x[h] + _dot(qm[h].astype(BF16), block_diag(x[h].astype(BF16)))

    for h in heads:
        k = k_ref[0, :, kcols[h]].astype(F32)
        kb = k * beta[h]
        rhs = jnp.concatenate([v_ref[0, :, vcols[h]].astype(F32) * beta[h], kb * eg[h]], axis=1)
        sol = rhs + _dot(block_diag(qm[h].astype(BF16)), rhs.astype(BF16))
        u_ref[h] = sol[:, :GDN_DV]
        q_dec = q_ref[0, :, kcols[h]].astype(F32) * eg[h]
        kd_t = (k * jnp.exp(gl[h] - g[h])).T
        ak = jnp.concatenate([a_c[h], kd_t], axis=0).astype(BF16)
        for c in range(NCHUNK):
            rows = slice(c * CHUNK, (c + 1) * CHUNK)
            lhs_ref[h, c] = jnp.concatenate([sol[rows, GDN_DV:], q_dec[rows]], axis=0).astype(BF16)
            ak_ref[h, c] = ak[:, rows]

    for c in range(NCHUNK):
        rows = slice(c * CHUNK, (c + 1) * CHUNK)
        r = [_dot(lhs_ref[h, c], s_ref[h].astype(BF16)) for h in heads]
        r2 = []
        for h in heads:
            v_new = (u_ref[h, rows, :] - r[h][:CHUNK]).astype(BF16)
            r2.append(_dot(ak_ref[h, c], v_new))
        for h in heads:
            tot = jnp.exp(gl[h][c * CHUNK:c * CHUNK + 1, :])
            s_ref[h] = s_ref[h] * tot + r2[h][CHUNK:]
            o = r[h][CHUNK:] + r2[h][:CHUNK]
            zc = z_ref[0, rows, vcols[h]].astype(F32)
            o_ref[0, rows, vcols[h]] = _head_norm_gate(o, zc, hn_w).astype(BF16)


def _gdn_scan(main, tail, alog_row, dtb_row, head_norm):
    bsz, lp, _ = main.shape
    qk = GDN_HEADS * GDN_DK
    vv = GDN_HEADS * GDN_DV
    return pl.pallas_call(
        _gdn_scan_kernel,
        grid=(bsz, lp // TBLK),
        in_specs=[
            pl.BlockSpec((1, TBLK, qk), lambda b, t: (b, t, 0)),
            pl.BlockSpec((1, TBLK, qk), lambda b, t: (b, t, 1)),
            pl.BlockSpec((1, TBLK, vv), lambda b, t: (b, t, 1)),
            pl.BlockSpec((1, TBLK, vv), lambda b, t: (b, t, 2)),
            pl.BlockSpec((1, TBLK, LANES), lambda b, t: (b, t, 0)),
            pl.BlockSpec((1, LANES), lambda b, t: (0, 0)),
            pl.BlockSpec((1, LANES), lambda b, t: (0, 0)),
            pl.BlockSpec((1, GDN_DV), lambda b, t: (0, 0)),
        ],
        out_specs=pl.BlockSpec((1, TBLK, vv), lambda b, t: (b, t, 0)),
        out_shape=jax.ShapeDtypeStruct((bsz, lp, vv), BF16),
        scratch_shapes=[
            pltpu.VMEM((GDN_HEADS, GDN_DK, GDN_DV), F32),
            pltpu.VMEM((GDN_HEADS, TBLK, GDN_DV), F32),
            pltpu.VMEM((GDN_HEADS, NCHUNK, 2 * CHUNK, GDN_DK), BF16),
            pltpu.VMEM((GDN_HEADS, NCHUNK, CHUNK + GDN_DK, CHUNK), BF16),
        ],
        compiler_params=pltpu.CompilerParams(
            dimension_semantics=("arbitrary", "arbitrary"), vmem_limit_bytes=VMEM_LIMIT),
        name="gdn_scan",
    )(main, main, main, main, tail, alog_row, dtb_row, head_norm)


def _split_w_in(w_in, n_main):
    tail = w_in[:, n_main:]
    tail = jnp.pad(tail, ((0, 0), (0, LANES - tail.shape[1])))
    return w_in[:, :n_main].astype(BF16), tail.astype(BF16)


def _lane_row(x, offset):
    return jnp.pad(x.astype(F32), (offset, LANES - offset - x.shape[0]))[None, :]


def kernel(x, meta_tokens, gla_norm, gla_w_in, gla_w_gk, gla_b_gk, gla_head_norm, gla_w_out,
           gdn_norm, gdn_w_in, gdn_conv_w, gdn_a_log, gdn_dt_bias, gdn_head_norm, gdn_w_out,
           final_norm):
    bsz, seq, d = x.shape
    assert seq % TBLK == 0 and (bsz * (LEAD + seq)) % ROW_TILE == 0
    lp = LEAD + seq
    lead = jnp.concatenate(
        [jnp.zeros((LEAD - N_META, d), x.dtype), meta_tokens.astype(x.dtype)], axis=0)
    h = jnp.concatenate([jnp.broadcast_to(lead[None], (bsz, LEAD, d)), x], axis=1)
    h2 = h.reshape(bsz * lp, d)

    depth = gla_w_in.shape[0] + gdn_w_in.shape[0]
    gla_main = 2 * GLA_HEADS * GLA_DK + 2 * GLA_HEADS * GLA_DV
    gdn_main = 2 * GDN_HEADS * GDN_DK + 2 * GDN_HEADS * GDN_DV
    for i in range(depth):
        j = i // 2
        fin = final_norm[None, :] if i == depth - 1 else None
        if i % 2 == 0:
            w_main, w_tail = _split_w_in(gla_w_in[j], gla_main)
            main, tail = _inproj(h2, gla_norm[j][None, :], w_main, w_tail)
            w_gk = jnp.pad(gla_w_gk[j], ((0, LANES - GLA_GATE_RANK), (0, 0))).astype(BF16)
            og = _gla_scan(main.reshape(bsz, lp, gla_main), tail.reshape(bsz, lp, LANES),
                           w_gk, gla_b_gk[j][None, :], gla_head_norm[j][None, :])
            h2 = _outproj(og.reshape(bsz * lp, -1), h2, gla_w_out[j].astype(BF16), fin)
        else:
            w_main, w_tail = _split_w_in(gdn_w_in[j], gdn_main)
            main, tail = _gdn_inproj(h2.reshape(bsz, lp, d), gdn_norm[j][None, :],
                                     w_main, w_tail, gdn_conv_w[j])
            og = _gdn_scan(main, tail, _lane_row(gdn_a_log[j], GDN_HEADS),
                           _lane_row(gdn_dt_bias[j], GDN_HEADS), gdn_head_norm[j][None, :])
            h2 = _outproj(og.reshape(bsz * lp, -1), h2, gdn_w_out[j].astype(BF16), fin)
    return h2.reshape(bsz, lp, d)[:, LEAD:]
```

```python
import functools

import jax
import jax.numpy as jnp
from jax import lax
from jax.experimental import pallas as pl
from jax.experimental.pallas import tpu as pltpu

F32 = jnp.float32
BF16 = jnp.bfloat16

N_META = 16
CHUNK = 64
NORM_EPS = 1e-6
GLA_HEADS = 4
GLA_DK = 128
GLA_DV = 256
GLA_GATE_RANK = 16
GLA_GATE_NORMALIZER = 16.0
GDN_HEADS = 8
GDN_DK = 128
GDN_DV = 256
CONV_WIDTH = 4

LANES = 128
SUBLANES = 8
TBLK = 256
NCHUNK = TBLK // CHUNK
ROW_TILE = 512
LEAD = ROW_TILE
COL_TILE = 512
EPI_ROWS = 64
VMEM_LIMIT = 56 * 1024 * 1024
NEUMANN_DOUBLINGS = 5


def _dot(a, b):
    return jnp.dot(a, b, preferred_element_type=F32)


def _dot_nt(a, b):
    return lax.dot_general(a, b, (((1,), (1,)), ((), ())), preferred_element_type=F32)


def _dot_tn(a, b):
    return lax.dot_general(a, b, (((0,), (0,)), ((), ())), preferred_element_type=F32)


def _softplus(x):
    return jnp.maximum(x, 0.0) + jnp.log1p(jnp.exp(-jnp.abs(x)))


def _silu(x):
    return x * jax.nn.sigmoid(x)


def _chunk_masks(n):
    row = lax.broadcasted_iota(jnp.int32, (n, n), 0)
    col = lax.broadcasted_iota(jnp.int32, (n, n), 1)
    same = (row // CHUNK) == (col // CHUNK)
    return same & (col <= row), same & (col < row)


def _chunk_cumsum(tri, x):
    x1 = x.astype(BF16)
    r = x - x1.astype(F32)
    x2 = r.astype(BF16)
    r = r - x2.astype(F32)
    x3 = r.astype(BF16)
    return _dot(tri, x1) + _dot(tri, x2) + _dot(tri, x3)


def _chunk_last(x):
    n = x.shape[0]
    parts = []
    for c in range(n // CHUNK):
        last = x[c * CHUNK + CHUNK - 1:c * CHUNK + CHUNK, :]
        parts.append(jnp.broadcast_to(last, (CHUNK, x.shape[1])))
    return jnp.concatenate(parts, axis=0)


def _lane_column(x, idx):
    lane = lax.broadcasted_iota(jnp.int32, x.shape, 1)
    return jnp.sum(jnp.where(lane == idx, x, 0.0), axis=1, keepdims=True)


def _valid_rows(t, n):
    rid = t * n + lax.broadcasted_iota(jnp.int32, (n, 1), 0)
    return rid >= (LEAD - N_META)


def _head_norm_gate(o, z, w):
    ms = jnp.mean(o * o, axis=-1, keepdims=True)
    return (o * lax.rsqrt(ms + NORM_EPS) * w) * _silu(z)


def _rms_to_bf16(h, w):
    ms = jnp.mean(h * h, axis=-1, keepdims=True)
    return (h * lax.rsqrt(ms + NORM_EPS) * w).astype(BF16)


def _h_specs(h_src):
    if len(h_src) == 2:
        x, lead = h_src
        return [pl.BlockSpec((1, ROW_TILE, x.shape[2]), lambda b, t: (b, jnp.maximum(t - 1, 0), 0)),
                pl.BlockSpec(lead.shape, lambda b, t: (0, 0))]
    (h,) = h_src
    return [pl.BlockSpec((1, ROW_TILE, h.shape[2]), lambda b, t: (b, t, 0))]


def _h_tile(h_refs):
    if len(h_refs) == 2:
        x_ref, lead_ref = h_refs
        return jnp.where(pl.program_id(1) == 0, lead_ref[...], x_ref[0])
    return h_refs[0][0]


def _h_dims(h_src):
    if len(h_src) == 2:
        x, lead = h_src
        return x.shape[0], lead.shape[0] + x.shape[1], x.shape[2]
    return h_src[0].shape


def _inproj_kernel(*refs, n_h):
    nw_ref, wm_ref, wt_ref, om_ref, ot_ref = refs[n_h:]
    hn = _rms_to_bf16(_h_tile(refs[:n_h]), nw_ref[...])
    for j in range(wm_ref.shape[1] // COL_TILE):
        cols = slice(j * COL_TILE, (j + 1) * COL_TILE)
        om_ref[0, :, cols] = _dot(hn, wm_ref[:, cols]).astype(BF16)
    ot_ref[0] = _dot(hn, wt_ref[...])


def _inproj(h_src, norm_w, w_main, w_tail):
    bsz, lp, d = _h_dims(h_src)
    n_main = w_main.shape[1]
    return pl.pallas_call(
        functools.partial(_inproj_kernel, n_h=len(h_src)),
        grid=(bsz, lp // ROW_TILE),
        in_specs=_h_specs(h_src) + [
            pl.BlockSpec((1, d), lambda b, t: (0, 0)),
            pl.BlockSpec((d, n_main), lambda b, t: (0, 0)),
            pl.BlockSpec((d, LANES), lambda b, t: (0, 0)),
        ],
        out_specs=[
            pl.BlockSpec((1, ROW_TILE, n_main), lambda b, t: (b, t, 0)),
            pl.BlockSpec((1, ROW_TILE, LANES), lambda b, t: (b, t, 0)),
        ],
        out_shape=[
            jax.ShapeDtypeStruct((bsz, lp, n_main), BF16),
            jax.ShapeDtypeStruct((bsz, lp, LANES), F32),
        ],
        compiler_params=pltpu.CompilerParams(
            dimension_semantics=("arbitrary", "arbitrary"), vmem_limit_bytes=VMEM_LIMIT),
        name="inproj",
    )(*h_src, norm_w, w_main, w_tail)


def _gdn_inproj_kernel(h_ref, nw_ref, wm_ref, wt_ref, cw_ref, om_ref, ot_ref, carry_ref,
                       pe_ref):
    t = pl.program_id(1)

    @pl.when(t == 0)
    def _():
        carry_ref[...] = jnp.zeros_like(carry_ref)

    hn = _rms_to_bf16(h_ref[0], nw_ref[...])
    n_conv = cw_ref.shape[1]
    n_qk = 2 * GDN_HEADS * GDN_DK
    n_tiles = wm_ref.shape[1] // COL_TILE
    n_vregs = (SUBLANES + EPI_ROWS) // SUBLANES
    sub = lax.broadcasted_iota(jnp.int32, (n_vregs, SUBLANES, COL_TILE), 1)

    def shift_rows(x3, s):
        r = pltpu.roll(x3, s, axis=1)
        prev = jnp.concatenate([r[-1:], r[:-1]], axis=0)
        return jnp.where(sub < s, prev, r)

    def conv_epilogue(slot, c0):
        cols = slice(c0, c0 + COL_TILE)
        w0, w1, w2, w3 = (cw_ref[j:j + 1, cols] for j in range(CONV_WIDTH))
        scale = GDN_DK ** -0.5 if c0 < n_qk // 2 else None
        for r0 in range(0, ROW_TILE, EPI_ROWS):
            ext = pe_ref[slot, r0:r0 + SUBLANES + EPI_ROWS, :]
            ext = ext.reshape(n_vregs, SUBLANES, COL_TILE)
            x1 = shift_rows(ext, 1)
            near = w3 * ext + w2 * x1
            far = w1 * ext + w0 * x1
            y = (near + shift_rows(far, 2)).reshape(SUBLANES + EPI_ROWS, COL_TILE)
            y = _silu(y[SUBLANES:, :])
            if c0 < n_qk:
                parts = []
                for i in range(COL_TILE // GDN_DK):
                    yh = y[:, i * GDN_DK:(i + 1) * GDN_DK]
                    yh = yh * lax.rsqrt(jnp.sum(yh * yh, axis=-1, keepdims=True) + NORM_EPS)
                    parts.append(yh * scale if scale is not None else yh)
                y = jnp.concatenate(parts, axis=1)
            om_ref[0, r0:r0 + EPI_ROWS, cols] = y.astype(BF16)

    def project(j):
        cols = slice(j * COL_TILE, (j + 1) * COL_TILE)
        p = _dot(hn, wm_ref[:, cols])
        if j * COL_TILE >= n_conv:
            om_ref[0, :, cols] = p.astype(BF16)
            return
        slot = j % 2
        pe_ref[slot, :SUBLANES, :] = carry_ref[:, cols]
        pe_ref[slot, SUBLANES:, :] = p
        carry_ref[:, cols] = p[ROW_TILE - SUBLANES:, :]

    n_conv_tiles = n_conv // COL_TILE
    project(0)
    for j in range(n_tiles):
        if j + 1 < n_tiles:
            project(j + 1)
        if j < n_conv_tiles:
            conv_epilogue(j % 2, j * COL_TILE)
    ot_ref[0] = _dot(hn, wt_ref[...])


def _gdn_inproj(h3, norm_w, w_main, w_tail, conv_w):
    bsz, lp, d = h3.shape
    n_main = w_main.shape[1]
    n_conv = conv_w.shape[1]
    return pl.pallas_call(
        _gdn_inproj_kernel,
        grid=(bsz, lp // ROW_TILE),
        in_specs=[
            pl.BlockSpec((1, ROW_TILE, d), lambda b, t: (b, t, 0)),
            pl.BlockSpec((1, d), lambda b, t: (0, 0)),
            pl.BlockSpec((d, n_main), lambda b, t: (0, 0)),
            pl.BlockSpec((d, LANES), lambda b, t: (0, 0)),
            pl.BlockSpec((CONV_WIDTH, n_conv), lambda b, t: (0, 0)),
        ],
        out_specs=[
            pl.BlockSpec((1, ROW_TILE, n_main), lambda b, t: (b, t, 0)),
            pl.BlockSpec((1, ROW_TILE, LANES), lambda b, t: (b, t, 0)),
        ],
        out_shape=[
            jax.ShapeDtypeStruct((bsz, lp, n_main), BF16),
            jax.ShapeDtypeStruct((bsz, lp, LANES), F32),
        ],
        scratch_shapes=[
            pltpu.VMEM((SUBLANES, n_conv), F32),
            pltpu.VMEM((2, SUBLANES + ROW_TILE, COL_TILE), F32),
        ],
        compiler_params=pltpu.CompilerParams(
            dimension_semantics=("arbitrary", "arbitrary"), vmem_limit_bytes=VMEM_LIMIT),
        name="gdn_inproj",
    )(h3, norm_w, w_main, w_tail, conv_w)


def _outproj_kernel(*refs, n_h, final):
    o_ref, w_ref = refs[n_h:n_h + 2]
    out_ref = refs[-1]
    h = _h_tile(refs[:n_h]) + _dot(o_ref[0], w_ref[...])
    if final:
        nw_ref = refs[n_h + 2]
        ms = jnp.mean(h * h, axis=-1, keepdims=True)
        h = h * lax.rsqrt(ms + NORM_EPS) * nw_ref[...]
    out_ref[0] = h


def _outproj(h_src, o3, w_out, final_norm=None):
    bsz, lp, d = _h_dims(h_src)
    v = o3.shape[2]
    final = final_norm is not None
    in_specs = _h_specs(h_src) + [
        pl.BlockSpec((1, ROW_TILE, v), lambda b, t: (b, t, 0)),
        pl.BlockSpec((v, d), lambda b, t: (0, 0)),
    ]
    args = [*h_src, o3, w_out]
    if final:
        in_specs.append(pl.BlockSpec((1, d), lambda b, t: (0, 0)))
        args.append(final_norm)
        out_spec = pl.BlockSpec((1, ROW_TILE, d), lambda b, t: (b, jnp.maximum(t - 1, 0), 0))
        out_shape = jax.ShapeDtypeStruct((bsz, lp - LEAD, d), F32)
    else:
        out_spec = pl.BlockSpec((1, ROW_TILE, d), lambda b, t: (b, t, 0))
        out_shape = jax.ShapeDtypeStruct((bsz, lp, d), F32)
    return pl.pallas_call(
        functools.partial(_outproj_kernel, n_h=len(h_src), final=final),
        grid=(bsz, lp // ROW_TILE),
        in_specs=in_specs,
        out_specs=out_spec,
        out_shape=out_shape,
        compiler_params=pltpu.CompilerParams(
            dimension_semantics=("arbitrary", "arbitrary"), vmem_limit_bytes=VMEM_LIMIT),
        name="outproj",
    )(*args)


def _gla_scan_kernel(q_ref, k_ref, v_ref, z_ref, tail_ref, wgk_ref, bgk_ref, hn_ref,
                     o_ref, st_ref, oi_ref, kv_ref):
    t = pl.program_id(1)

    @pl.when(t == 0)
    def _():
        st_ref[...] = jnp.zeros_like(st_ref)

    causal, _ = _chunk_masks(TBLK)
    tri = jnp.where(causal, 1.0, 0.0).astype(BF16)

    gk = _dot(tail_ref[0].astype(BF16), wgk_ref[...]) + bgk_ref[...]
    log_a = -_softplus(-gk) / GLA_GATE_NORMALIZER
    log_a = jnp.where(_valid_rows(t, TBLK), log_a, 0.0)
    b = _chunk_cumsum(tri, log_a)
    b_last = _chunk_last(b)
    hn_w = hn_ref[...]

    heads = range(GLA_HEADS)
    kcols = [slice(h * GLA_DK, (h + 1) * GLA_DK) for h in heads]
    vcols = [slice(h * GLA_DV, (h + 1) * GLA_DV) for h in heads]
    q_in, k_out, scores = [], [], []
    for h in heads:
        bh = b[:, kcols[h]]
        qf = q_ref[0, :, kcols[h]].astype(F32) * (GLA_DK ** -0.5)
        kf = k_ref[0, :, kcols[h]].astype(F32)
        q_in.append((qf * jnp.exp(bh)).astype(BF16))
        k_in = (kf * jnp.exp(-bh)).astype(BF16)
        k_out.append((kf * jnp.exp(b_last[:, kcols[h]] - bh)).astype(BF16))
        scores.append(jnp.where(causal, _dot_nt(q_in[h], k_in), 0.0).astype(BF16))
    for h in heads:
        vh = v_ref[0, :, vcols[h]]
        oi_ref[h] = _dot(scores[h], vh)
        for c in range(NCHUNK):
            rows = slice(c * CHUNK, (c + 1) * CHUNK)
            kv_ref[h, c] = _dot_tn(vh[rows], k_out[h][rows])
    for c in range(NCHUNK):
        rows = slice(c * CHUNK, (c + 1) * CHUNK)
        for h in heads:
            st = st_ref[h]
            o = oi_ref[h, rows, :] + _dot_nt(q_in[h][rows], st.astype(BF16))
            st_ref[h] = st * jnp.exp(b_last[c * CHUNK:c * CHUNK + 1, kcols[h]]) + kv_ref[h, c]
            zc = z_ref[0, rows, vcols[h]].astype(F32)
            o_ref[0, rows, vcols[h]] = _head_norm_gate(o, zc, hn_w).astype(BF16)


def _gla_scan(main, tail, w_gk, b_gk, head_norm):
    bsz, lp, _ = main.shape
    qk = GLA_HEADS * GLA_DK
    vv = GLA_HEADS * GLA_DV
    return pl.pallas_call(
        _gla_scan_kernel,
        grid=(bsz, lp // TBLK),
        in_specs=[
            pl.BlockSpec((1, TBLK, qk), lambda b, t: (b, t, 0)),
            pl.BlockSpec((1, TBLK, qk), lambda b, t: (b, t, 1)),
            pl.BlockSpec((1, TBLK, vv), lambda b, t: (b, t, 1)),
            pl.BlockSpec((1, TBLK, vv), lambda b, t: (b, t, 2)),
            pl.BlockSpec((1, TBLK, LANES), lambda b, t: (b, t, 0)),
            pl.BlockSpec((LANES, qk), lambda b, t: (0, 0)),
            pl.BlockSpec((1, qk), lambda b, t: (0, 0)),
            pl.BlockSpec((1, GLA_DV), lambda b, t: (0, 0)),
        ],
        out_specs=pl.BlockSpec((1, TBLK, vv), lambda b, t: (b, t, 0)),
        out_shape=jax.ShapeDtypeStruct((bsz, lp, vv), BF16),
        scratch_shapes=[
            pltpu.VMEM((GLA_HEADS, GLA_DV, GLA_DK), F32),
            pltpu.VMEM((GLA_HEADS, TBLK, GLA_DV), F32),
            pltpu.VMEM((GLA_HEADS, NCHUNK, GLA_DV, GLA_DK), F32),
        ],
        compiler_params=pltpu.CompilerParams(
            dimension_semantics=("arbitrary", "arbitrary"), vmem_limit_bytes=VMEM_LIMIT),
        name="gla_scan",
    )(main, main, main, main, tail, w_gk, b_gk, head_norm)


def _gdn_scan_kernel(q_ref, k_ref, v_ref, z_ref, tail_ref, alog_ref, dtb_ref, hn_ref,
                     o_ref, s_ref, u_ref, lhs_ref, ak_ref):
    t = pl.program_id(1)

    @pl.when(t == 0)
    def _():
        s_ref[...] = jnp.zeros_like(s_ref)

    causal, _ = _chunk_masks(TBLK)
    tri = jnp.where(causal, 1.0, 0.0).astype(BF16)
    ci = lax.broadcasted_iota(jnp.int32, (CHUNK, TBLK), 0)
    lane = lax.broadcasted_iota(jnp.int32, (CHUNK, TBLK), 1)
    cj = lane % CHUNK
    cblk = lane // CHUNK
    in_blk = [cblk == c for c in range(NCHUNK)]
    le = cj <= ci
    lt = cj < ci

    def compact(blocks):
        out = blocks[NCHUNK - 1]
        for c in range(NCHUNK - 2, -1, -1):
            out = jnp.where(in_blk[c], blocks[c], out)
        return out

    def block_diag(xc):
        zero = jnp.zeros_like(xc)
        return jnp.concatenate([jnp.where(in_blk[c], xc, zero) for c in range(NCHUNK)], axis=0)

    def row_blocks(x):
        return [x[c * CHUNK:(c + 1) * CHUNK] for c in range(NCHUNK)]

    tl = tail_ref[0]
    beta_all = jax.nn.sigmoid(tl)
    la_all = -jnp.exp(alog_ref[...]) * _softplus(tl + dtb_ref[...])
    la_all = jnp.where(_valid_rows(t, TBLK), la_all, 0.0)
    g_all = _chunk_cumsum(tri, la_all)
    gl_all = _chunk_last(g_all)
    g_t = g_all.T
    hn_w = hn_ref[...]

    heads = range(GDN_HEADS)
    kcols = [slice(h * GDN_DK, (h + 1) * GDN_DK) for h in heads]
    vcols = [slice(h * GDN_DV, (h + 1) * GDN_DV) for h in heads]
    beta = [_lane_column(beta_all, h) for h in heads]
    g = [_lane_column(g_all, GDN_HEADS + h) for h in heads]
    gl = [_lane_column(gl_all, GDN_HEADS + h) for h in heads]
    eg = [jnp.exp(g[h]) for h in heads]

    x, a_c = [], []
    for h in heads:
        k_b16 = k_ref[0, :, kcols[h]]
        kb = k_b16.astype(F32) * beta[h]
        kq = jnp.concatenate([kb.astype(BF16), q_ref[0, :, kcols[h]]], axis=0)
        kk = _dot_nt(kq, k_b16)
        g_row = g_t[GDN_HEADS + h:GDN_HEADS + h + 1, :]
        g_c = compact(row_blocks(g[h]))
        decay = jnp.exp(jnp.where(le, g_c - g_row, -jnp.inf))
        x.append(-jnp.where(lt, compact(row_blocks(kk[:TBLK])) * decay, 0.0))
        a_c.append(compact(row_blocks(kk[TBLK:])) * decay)

    qm = list(x)
    for h in heads:
        xb = x[h].astype(BF16)
        x[h] = _dot(xb, block_diag(xb))
    for _ in range(NEUMANN_DOUBLINGS - 1):
        for h in heads:
            xb = x[h].astype(BF16)
            r = _dot(jnp.concatenate([qm[h].astype(BF16), xb], axis=0), block_diag(xb))
            qm[h] = qm[h] + x[h] + r[:CHUNK]
            x[h] = r[CHUNK:]
    for h in heads:
        qm[h] = qm[h] + x[h] + _dot(qm[h].astype(BF16), block_diag(x[h].astype(BF16)))

    for h in heads:
        k = k_ref[0, :, kcols[h]].astype(F32)
        kb = k * beta[h]
        rhs = jnp.concatenate([v_ref[0, :, vcols[h]].astype(F32) * beta[h], kb * eg[h]], axis=1)
        sol = rhs + _dot(block_diag(qm[h].astype(BF16)), rhs.astype(BF16))
        u_ref[h] = sol[:, :GDN_DV]
        q_dec = q_ref[0, :, kcols[h]].astype(F32) * eg[h]
        kd_t = (k * jnp.exp(gl[h] - g[h])).T
        ak = jnp.concatenate([a_c[h], kd_t], axis=0).astype(BF16)
        for c in range(NCHUNK):
            rows = slice(c * CHUNK, (c + 1) * CHUNK)
            lhs_ref[h, c] = jnp.concatenate([sol[rows, GDN_DV:], q_dec[rows]], axis=0).astype(BF16)
            ak_ref[h, c] = ak[:, rows]

    for c in range(NCHUNK):
        rows = slice(c * CHUNK, (c + 1) * CHUNK)
        r = [_dot(lhs_ref[h, c], s_ref[h].astype(BF16)) for h in heads]
        r2 = []
        for h in heads:
            v_new = (u_ref[h, rows, :] - r[h][:CHUNK]).astype(BF16)
            r2.append(_dot(ak_ref[h, c], v_new))
        for h in heads:
            tot = jnp.exp(gl[h][c * CHUNK:c * CHUNK + 1, :])
            s_ref[h] = s_ref[h] * tot + r2[h][CHUNK:]
            o = r[h][CHUNK:] + r2[h][:CHUNK]
            zc = z_ref[0, rows, vcols[h]].astype(F32)
            o_ref[0, rows, vcols[h]] = _head_norm_gate(o, zc, hn_w).astype(BF16)


def _gdn_scan(main, tail, alog_row, dtb_row, head_norm):
    bsz, lp, _ = main.shape
    qk = GDN_HEADS * GDN_DK
    vv = GDN_HEADS * GDN_DV
    return pl.pallas_call(
        _gdn_scan_kernel,
        grid=(bsz, lp // TBLK),
        in_specs=[
            pl.BlockSpec((1, TBLK, qk), lambda b, t: (b, t, 0)),
            pl.BlockSpec((1, TBLK, qk), lambda b, t: (b, t, 1)),
            pl.BlockSpec((1, TBLK, vv), lambda b, t: (b, t, 1)),
            pl.BlockSpec((1, TBLK, vv), lambda b, t: (b, t, 2)),
            pl.BlockSpec((1, TBLK, LANES), lambda b, t: (b, t, 0)),
            pl.BlockSpec((1, LANES), lambda b, t: (0, 0)),
            pl.BlockSpec((1, LANES), lambda b, t: (0, 0)),
            pl.BlockSpec((1, GDN_DV), lambda b, t: (0, 0)),
        ],
        out_specs=pl.BlockSpec((1, TBLK, vv), lambda b, t: (b, t, 0)),
        out_shape=jax.ShapeDtypeStruct((bsz, lp, vv), BF16),
        scratch_shapes=[
            pltpu.VMEM((GDN_HEADS, GDN_DK, GDN_DV), F32),
            pltpu.VMEM((GDN_HEADS, TBLK, GDN_DV), F32),
            pltpu.VMEM((GDN_HEADS, NCHUNK, 2 * CHUNK, GDN_DK), BF16),
            pltpu.VMEM((GDN_HEADS, NCHUNK, CHUNK + GDN_DK, CHUNK), BF16),
        ],
        compiler_params=pltpu.CompilerParams(
            dimension_semantics=("arbitrary", "arbitrary"), vmem_limit_bytes=VMEM_LIMIT),
        name="gdn_scan",
    )(main, main, main, main, tail, alog_row, dtb_row, head_norm)


def _split_w_in(w_in, n_main):
    tail = w_in[:, n_main:]
    tail = jnp.pad(tail, ((0, 0), (0, LANES - tail.shape[1])))
    return w_in[:, :n_main].astype(BF16), tail.astype(BF16)


def _lane_row(x, offset):
    return jnp.pad(x.astype(F32), (offset, LANES - offset - x.shape[0]))[None, :]


def kernel(x, meta_tokens, gla_norm, gla_w_in, gla_w_gk, gla_b_gk, gla_head_norm, gla_w_out,
           gdn_norm, gdn_w_in, gdn_conv_w, gdn_a_log, gdn_dt_bias, gdn_head_norm, gdn_w_out,
           final_norm):
    bsz, seq, d = x.shape
    assert seq % ROW_TILE == 0 and ROW_TILE % TBLK == 0 and LEAD == ROW_TILE
    lead = jnp.concatenate(
        [jnp.zeros((LEAD - N_META, d), x.dtype), meta_tokens.astype(x.dtype)], axis=0)
    h_src = (x, lead)

    depth = gla_w_in.shape[0] + gdn_w_in.shape[0]
    gla_main = 2 * GLA_HEADS * GLA_DK + 2 * GLA_HEADS * GLA_DV
    gdn_main = 2 * GDN_HEADS * GDN_DK + 2 * GDN_HEADS * GDN_DV
    for i in range(depth):
        j = i // 2
        fin = final_norm[None, :] if i == depth - 1 else None
        if i % 2 == 0:
            w_main, w_tail = _split_w_in(gla_w_in[j], gla_main)
            main, tail = _inproj(h_src, gla_norm[j][None, :], w_main, w_tail)
            w_gk = jnp.pad(gla_w_gk[j], ((0, LANES - GLA_GATE_RANK), (0, 0))).astype(BF16)
            og = _gla_scan(main, tail, w_gk, gla_b_gk[j][None, :], gla_head_norm[j][None, :])
            h_src = (_outproj(h_src, og, gla_w_out[j].astype(BF16), fin),)
        else:
            assert len(h_src) == 1
            w_main, w_tail = _split_w_in(gdn_w_in[j], gdn_main)
            main, tail = _gdn_inproj(h_src[0], gdn_norm[j][None, :], w_main, w_tail, gdn_conv_w[j])
            og = _gdn_scan(main, tail, _lane_row(gdn_a_log[j], GDN_HEADS),
                           _lane_row(gdn_dt_bias[j], GDN_HEADS), gdn_head_norm[j][None, :])
            h_src = (_outproj(h_src, og, gdn_w_out[j].astype(BF16), fin),)
    return h_src[0]
```
